```python
import math
import jax
import jax.numpy as jnp
from jax import lax
import numpy as np

D_MODEL = 1024
BATCH = 8
SEQ = 4096
DEPTH = 4

N_BRANCHES = 3
NORM_EPS = 1e-6
D_HYENA = 512
HYENA_EMB_DIM = 33
HYENA_FILTER_ORDER = 64
HYENA_FAST_DECAY_PCT = 0.3
HYENA_SLOW_DECAY_PCT = 1.5
HYENA_DECAY_TARGET = 1e-2
GLA_HEADS = 4
GLA_KEY_DIM = 512
GLA_VALUE_DIM = 512
GLA_HEAD_K = GLA_KEY_DIM // GLA_HEADS
GLA_HEAD_V = GLA_VALUE_DIM // GLA_HEADS
GLA_GATE_RANK = 16
GLA_GATE_NORMALIZER = 16.0
GLA_LOG_GATE_MIN = -1.0
GLA_CHUNK = 64
D_POOL = 512
POOL_WINDOWS = (2, 4, 8, 16)
POOL_GROUP = D_POOL // len(POOL_WINDOWS)
D_FF = 2816
IN_SPLITS = (3 * D_HYENA, GLA_KEY_DIM, GLA_KEY_DIM, GLA_VALUE_DIM, GLA_VALUE_DIM,
             2 * GLA_GATE_RANK, D_POOL, N_BRANCHES * D_MODEL)
D_IN = 3 * D_HYENA + 2 * GLA_KEY_DIM + 2 * GLA_VALUE_DIM + 2 * GLA_GATE_RANK + D_POOL + N_BRANCHES * D_MODEL

kernel_name = "hybrid_hyena_gla_pool_encoder"


def _split(t, sizes):
    out, off = [], 0
    for s in sizes:
        out.append(t[..., off:off + s])
        off += s
    return out


def _rms_norm(x, g):
    xf = x.astype(jnp.float32)
    y = xf * lax.rsqrt(jnp.mean(xf * xf, axis=-1, keepdims=True) + NORM_EPS)
    return (y * g.astype(jnp.float32)).astype(x.dtype)


def _dwconv3_centred(x, w, b):
    L = x.shape[1]
    xp = jnp.pad(x, ((0, 0), (1, 1), (0, 0)))
    return xp[:, :L] * w[0] + xp[:, 1:L + 1] * w[1] + xp[:, 2:] * w[2] + b


def _hyena_positional_features(L):
    t = jnp.linspace(0.0, 1.0, L, dtype=jnp.float32)[:, None]
    bands = (HYENA_EMB_DIM - 1) // 2
    w = 2.0 * math.pi * jnp.arange(L, dtype=jnp.float32)[:, None] / L
    f = jnp.linspace(1e-4, bands - 1, bands, dtype=jnp.float32)[None, :]
    ang = f * w
    z = jnp.concatenate([t, jnp.cos(ang), -jnp.sin(ang)], axis=-1)
    return t, z


def _hyena_filter(t, z, w1, b1, fr1, w2, b2, fr2, w3):
    L = t.shape[0]
    h = jnp.sin(fr1 * (z @ w1 + b1))
    h = jnp.sin(fr2 * (h @ w2 + b2))
    h = (h @ w3).astype(jnp.float32)
    max_decay = math.log(HYENA_DECAY_TARGET) / HYENA_FAST_DECAY_PCT
    min_decay = math.log(HYENA_DECAY_TARGET) / HYENA_SLOW_DECAY_PCT
    deltas = jnp.linspace(min_decay, max_decay, D_HYENA, dtype=jnp.float32)
    decay = jnp.exp(-t * jnp.abs(deltas))
    h_fwd = h[:, :D_HYENA] * decay
    h_bwd = h[:, D_HYENA:] * decay
    zero = jnp.zeros((1, D_HYENA), jnp.float32)
    return jnp.concatenate([h_fwd, zero, h_bwd[:0:-1]], axis=0)


def _hyena_mixer(u, conv_w, conv_b, t, z, w1, b1, fr1, w2, b2, fr2, w3, bias):
    B, L, _ = u.shape
    u = _dwconv3_centred(u, conv_w, conv_b)
    x0, x1, v = _split(u, (D_HYENA, D_HYENA, D_HYENA))
    zv = (x1 * v).astype(jnp.float32)
    k = _hyena_filter(t, z, w1, b1, fr1, w2, b2, fr2, w3)
    Zf = jnp.fft.rfft(zv, n=2 * L, axis=1)
    Kf = jnp.fft.rfft(k, n=2 * L, axis=0)
    y = jnp.fft.irfft(Zf * Kf[None], n=2 * L, axis=1)[:, :L]
    y = y + zv * bias.astype(jnp.float32)
    return (x0.astype(jnp.float32) * y).astype(u.dtype)


def _gla_direction(q, k, v, log_a, include_diag):
    B, L, H, DK = q.shape
    DV = v.shape[-1]
    C = GLA_CHUNK
    N = L // C
    q = q.reshape(B, N, C, H, DK)
    k = k.reshape(B, N, C, H, DK)
    v = v.reshape(B, N, C, H, DV)
    b = jnp.cumsum(log_a.reshape(B, N, C, H, DK), axis=2)
    b_last = b[:, :, -1]
    q_e = q * jnp.exp(b)
    k_e = k * jnp.exp(-b)
    scores = jnp.einsum('bnthd,bnshd->bnhts', q_e, k_e)
    mask = jnp.tril(jnp.ones((C, C), dtype=bool), k=0 if include_diag else -1)
    scores = jnp.where(mask, scores, 0.0)
    o_intra = jnp.einsum('bnhts,bnshe->bnthe', scores, v)
    k_dec = k * jnp.exp(b_last[:, :, None] - b)
    U = jnp.einsum('bnshd,bnshe->bnhde', k_dec, v)
    decay = jnp.exp(b_last)

    def step(S, inp):
        dec, inc = inp
        return dec[..., None] * S + inc, S

    S0 = jnp.zeros((B, H, DK, DV), U.dtype)
    _, S_prev = lax.scan(step, S0, (jnp.moveaxis(decay, 1, 0), jnp.moveaxis(U, 1, 0)))
    S_prev = jnp.moveaxis(S_prev, 0, 1)
    o_inter = jnp.einsum('bnthd,bnhde->bnthe', q_e, S_prev)
    return (o_intra + o_inter).reshape(B, L, H, DV)


def _gla_mixer(q, k, v, g, gate_lr, gate_w2, gate_b, norm_g):
    B, L, _ = q.shape
    q = q.reshape(B, L, GLA_HEADS, GLA_HEAD_K) * (GLA_HEAD_K ** -0.5)
    k = k.reshape(B, L, GLA_HEADS, GLA_HEAD_K)
    v = v.reshape(B, L, GLA_HEADS, GLA_HEAD_V)
    lr = gate_lr.reshape(B, L, 2, GLA_GATE_RANK)
    gk = jnp.einsum('blzr,zrd->blzd', lr, gate_w2) + gate_b
    log_a = jnp.maximum(jax.nn.log_sigmoid(gk.astype(jnp.float32)) / GLA_GATE_NORMALIZER,
                        GLA_LOG_GATE_MIN).reshape(B, L, 2, GLA_HEADS, GLA_HEAD_K)
    o_fwd = _gla_direction(q, k, v, log_a[:, :, 0], True)
    fl = lambda a: jnp.flip(a, axis=1)
    o_bwd = fl(_gla_direction(fl(q), fl(k), fl(v), fl(log_a[:, :, 1]), False))
    o = _rms_norm(o_fwd + o_bwd, norm_g)
    return o.reshape(B, L, GLA_VALUE_DIM).astype(g.dtype) * jax.nn.silu(g)


def _pool_mixer(u, pool_w, pool_scale):
    B, L, _ = u.shape
    pos = jnp.arange(L)
    groups = []
    for gi, w in enumerate(POOL_WINDOWS):
        xg = u[..., gi * POOL_GROUP:(gi + 1) * POOL_GROUP].astype(jnp.float32)
        half = w // 2
        xp = jnp.pad(xg, ((0, 0), (half, half), (0, 0)))
        cs = jnp.pad(jnp.cumsum(xp, axis=1), ((0, 0), (1, 0), (0, 0)))
        wsum = cs[:, w:w + L] - cs[:, :L]
        count = (jnp.minimum(pos + half, L) - jnp.maximum(pos - half, 0)).astype(jnp.float32)
        groups.append(wsum / count[None, :, None] - xg)
    d = jnp.stack(groups, axis=2)
    y = jnp.einsum('blgc,gcd->blgd', d, pool_w.astype(jnp.float32)).reshape(B, L, D_POOL)
    return (y * pool_scale.astype(jnp.float32)).astype(u.dtype)


def setup_inputs(seed: int = 0) -> dict:
    key = jax.random.key(seed)
    ks = iter(jax.random.split(key, 40))
    f32 = jnp.float32
    n = DEPTH
    FO = HYENA_FILTER_ORDER

    def nrm(shape, scale):
        return jax.random.normal(next(ks), shape, f32) * scale

    def gain(shape, noise=0.05):
        return 1.0 + nrm(shape, noise)

    return {
        "x": nrm((BATCH, SEQ, D_MODEL), 1.0),
        "norm_mix_pre": gain((n, D_MODEL)),
        "norm_mix_post": gain((n, D_MODEL)),
        "norm_ffn_pre": gain((n, D_MODEL)),
        "norm_ffn_post": gain((n, D_MODEL)),
        "w_in": nrm((n, D_MODEL, D_IN), D_MODEL ** -0.5),
        "hy_conv_w": nrm((n, 3, 3 * D_HYENA), 3 ** -0.5),
        "hy_conv_b": nrm((n, 3 * D_HYENA), 0.02),
        "hy_filt_w1": nrm((n, HYENA_EMB_DIM, FO), HYENA_EMB_DIM ** -0.5),
        "hy_filt_b1": nrm((n, FO), 0.1),
        "hy_filt_freq1": gain((n, FO), 0.01),
        "hy_filt_w2": nrm((n, FO, FO), FO ** -0.5),
        "hy_filt_b2": nrm((n, FO), 0.1),
        "hy_filt_freq2": gain((n, FO), 0.01),
        "hy_filt_w3": nrm((n, FO, 2 * D_HYENA), 0.005),
        "hy_bias": nrm((n, D_HYENA), 1.0),
        "gla_gate_w2": nrm((n, 2, GLA_GATE_RANK, GLA_KEY_DIM), GLA_GATE_RANK ** -0.5),
        "gla_gate_b": nrm((n, 2, GLA_KEY_DIM), 0.1),
        "gla_norm": gain((n, GLA_HEAD_V), 0.01),
        "pool_w": nrm((n, len(POOL_WINDOWS), POOL_GROUP, POOL_GROUP), POOL_GROUP ** -0.5),
        "pool_scale": gain((n, D_POOL), 0.1),
        "w_br_hyena": nrm((n, D_HYENA, D_MODEL), D_HYENA ** -0.5),
        "w_br_gla": nrm((n, GLA_VALUE_DIM, D_MODEL), GLA_VALUE_DIM ** -0.5),
        "w_br_pool": nrm((n, D_POOL, D_MODEL), D_POOL ** -0.5),
        "w_out": nrm((n, D_MODEL, D_MODEL), D_MODEL ** -0.5),
        "ffn_w_up": nrm((n, D_MODEL, 2 * D_FF), D_MODEL ** -0.5),
        "ffn_conv_w": nrm((n, 3, D_FF), 3 ** -0.5),
        "ffn_conv_b": nrm((n, D_FF), 0.02),
        "ffn_w_down": nrm((n, D_FF, D_MODEL), D_FF ** -0.5),
    }


def reference(x, norm_mix_pre, norm_mix_post, norm_ffn_pre, norm_ffn_post, w_in,
              hy_conv_w, hy_conv_b, hy_filt_w1, hy_filt_b1, hy_filt_freq1, hy_filt_w2,
              hy_filt_b2, hy_filt_freq2, hy_filt_w3, hy_bias, gla_gate_w2, gla_gate_b,
              gla_norm, pool_w, pool_scale, w_br_hyena, w_br_gla, w_br_pool, w_out,
              ffn_w_up, ffn_conv_w, ffn_conv_b, ffn_w_down):
    B, L, _ = x.shape
    t_pos, z_pos = _hyena_positional_features(L)
    for i in range(DEPTH):
        h = _rms_norm(x, norm_mix_pre[i])
        proj = h @ w_in[i]
        u_hy, q, k, v, g_out, gate_lr, u_pool, gate_logits = _split(proj, IN_SPLITS)
        y_hy = _hyena_mixer(u_hy, hy_conv_w[i], hy_conv_b[i], t_pos, z_pos,
                            hy_filt_w1[i], hy_filt_b1[i], hy_filt_freq1[i], hy_filt_w2[i],
                            hy_filt_b2[i], hy_filt_freq2[i], hy_filt_w3[i], hy_bias[i])
        y_gla = _gla_mixer(q, k, v, g_out, gate_lr, gla_gate_w2[i], gla_gate_b[i], gla_norm[i])
        y_pool = _pool_mixer(u_pool, pool_w[i], pool_scale[i])
        gates = jax.nn.sigmoid(gate_logits).reshape(B, L, N_BRANCHES, D_MODEL)
        merged = (gates[:, :, 0] * (y_hy @ w_br_hyena[i])
                  + gates[:, :, 1] * (y_gla @ w_br_gla[i])
                  + gates[:, :, 2] * (y_pool @ w_br_pool[i]))
        x = x + _rms_norm(merged @ w_out[i], norm_mix_post[i])
        h = _rms_norm(x, norm_ffn_pre[i])
        a, b = _split(h @ ffn_w_up[i], (D_FF, D_FF))
        a = _dwconv3_centred(a, ffn_conv_w[i], ffn_conv_b[i])
        y = (jax.nn.gelu(a, approximate=False) * b) @ ffn_w_down[i]
        x = x + _rms_norm(y, norm_ffn_post[i])
    return x
```

```python
import functools
import math

import numpy as np
import jax
import jax.numpy as jnp
from jax import lax
from jax.experimental import pallas as pl
from jax.experimental.pallas import tpu as pltpu

F32 = jnp.float32
BF16 = jnp.bfloat16

NORM_EPS = 1e-6
N_BRANCHES = 3
D_HYENA = 512
HYENA_EMB_DIM = 33
HYENA_FAST_DECAY_PCT = 0.3
HYENA_SLOW_DECAY_PCT = 1.5
HYENA_DECAY_TARGET = 1e-2
GLA_HEADS = 4
GLA_KEY_DIM = 512
GLA_VALUE_DIM = 512
GLA_HEAD_K = GLA_KEY_DIM // GLA_HEADS
GLA_HEAD_V = GLA_VALUE_DIM // GLA_HEADS
GLA_GATE_RANK = 16
GLA_GATE_NORMALIZER = 16.0
GLA_LOG_GATE_MIN = -1.0
GLA_CHUNK = 64
D_POOL = 512
POOL_WINDOWS = (2, 4, 8, 16)
POOL_GROUP = D_POOL // len(POOL_WINDOWS)
POOL_HALO = 16

LANES = 128
FILTER_PAD = 128
VMEM_LIMIT = 56 * 1024 * 1024


def _cparams(sem):
    return pltpu.CompilerParams(dimension_semantics=sem, vmem_limit_bytes=VMEM_LIMIT)


def _resident(block_shape, index_map):
    return pl.BlockSpec(block_shape, index_map, pipeline_mode=pl.Buffered(1))


def _rms(x, g):
    return x * lax.rsqrt(jnp.mean(x * x, axis=-1, keepdims=True) + NORM_EPS) * g


def _dot(a, b):
    return jnp.dot(a, b, preferred_element_type=F32)


def _inproj_kernel(x_ref, g_ref, w_hy, w_qkvg, w_lr, w_pool, w_gate,
                   o_hy, o_qkvg, o_lr, o_pool, o_gate):
    hb = _rms(x_ref[...], g_ref[...]).astype(BF16)
    o_hy[...] = _dot(hb, w_hy[...]).astype(o_hy.dtype)
    o_qkvg[...] = _dot(hb, w_qkvg[...]).astype(o_qkvg.dtype)
    o_lr[...] = _dot(hb, w_lr[...]).astype(o_lr.dtype)
    o_pool[...] = _dot(hb, w_pool[...]).astype(o_pool.dtype)
    o_gate[...] = _dot(hb, w_gate[...]).astype(o_gate.dtype)


def _inproj(x2, g, w_in, tm):
    T, D = x2.shape
    n_hy, n_qkvg, n_lr, n_pool, n_gate = (3 * D_HYENA, 2 * GLA_KEY_DIM + 2 * GLA_VALUE_DIM,
                                          2 * GLA_GATE_RANK, D_POOL, N_BRANCHES * D)
    offs = np.cumsum([0, n_hy, n_qkvg, n_lr, n_pool, n_gate])
    wb = w_in.astype(BF16)
    ws = [wb[:, offs[k]:offs[k + 1]] for k in range(5)]
    widths = [n_hy, n_qkvg, n_lr, n_pool, n_gate]
    dtypes = [BF16, BF16, F32, BF16, BF16]
    row = lambda i: (i, 0)
    const = lambda i: (0, 0)
    return pl.pallas_call(
        _inproj_kernel,
        grid=(T // tm,),
        in_specs=[pl.BlockSpec((tm, D), row), _resident((1, D), const)]
        + [_resident((D, n), const) for n in widths],
        out_specs=[pl.BlockSpec((tm, n), row) for n in widths],
        out_shape=[jax.ShapeDtypeStruct((T, n), dt) for n, dt in zip(widths, dtypes)],
        compiler_params=_cparams(("parallel",)),
        name="inproj",
    )(x2, g.reshape(1, D), *ws)


def _dft_constants(P):
    f = np.arange(P, dtype=np.int64)[:, None]
    n = np.arange(2 * P, dtype=np.int64)[None, :]
    ang = 2.0 * np.pi * (((2 * f + 1) * n) % (4 * P)).astype(np.float64) / (4 * P)
    fwd = np.concatenate([np.cos(ang), -np.sin(ang)], axis=0)
    fa = fwd[:, :P]
    fb = fwd[:, P:].copy()
    fb[:, 0] = 0.0
    inv = np.concatenate([np.cos(ang[:, :P]).T, -np.sin(ang[:, :P]).T], axis=1) / P
    return fa.astype(np.float32), fb.astype(np.float32), inv.astype(np.float32)


def _filter_mlp_kernel(z_ref, w1, b1, fr1, w2, b2, fr2, o_ref):
    hp = lax.Precision.HIGHEST
    h = jnp.sin(fr1[0] * (jnp.dot(z_ref[...], w1[0], precision=hp, preferred_element_type=F32) + b1[0]))
    h = jnp.sin(fr2[0] * (jnp.dot(h, w2[0], precision=hp, preferred_element_type=F32) + b2[0]))
    o_ref[0] = h


def _filter_spectrum_kernel(h_ref, t_ref, dl_ref, w3f, w3b, fa_ref, fb_ref, o_ref, kap_scr, *, L, P):
    hp = lax.Precision.HIGHEST
    h = h_ref[0]
    hf = jnp.dot(h, w3f[0], precision=hp, preferred_element_type=F32)
    hb = jnp.dot(h, w3b[0], precision=hp, preferred_element_type=F32)
    row = lax.broadcasted_iota(jnp.int32, (2 * L, 1), 0)
    decay = jnp.exp(-t_ref[...] * jnp.abs(dl_ref[...]))
    kap = jnp.where(row >= L, hf, hb) * decay
    kap_scr[...] = jnp.where(row == 0, 0.0, kap).astype(BF16)
    n_seg = 2 * L // P
    for e in range(1, n_seg):
        o_ref[0, e - 1] = (_dot(fa_ref[...], kap_scr[e * P:(e + 1) * P, :])
                           - _dot(fb_ref[...], kap_scr[(e - 1) * P:e * P, :]))


def _hyena_spectra(t_pos, z_pos, w1, b1, fr1, w2, b2, fr2, w3, L, P, ct):
    n, emb, fo = w1.shape
    C = D_HYENA
    lag = jnp.clip(jnp.abs(jnp.arange(2 * L) - L), 0, L - 1)
    z_lag = jnp.pad(z_pos[lag], ((0, 0), (0, FILTER_PAD - emb)))
    t_lag = t_pos[lag]
    pad_v = lambda a: jnp.pad(a, ((0, 0), (0, FILTER_PAD - fo))).reshape(n, 1, FILTER_PAD)
    w1p = jnp.pad(w1, ((0, 0), (0, FILTER_PAD - emb), (0, FILTER_PAD - fo)))
    w2p = jnp.pad(w2, ((0, 0), (0, FILTER_PAD - fo), (0, FILTER_PAD - fo)))
    w3p = jnp.pad(w3, ((0, 0), (0, FILTER_PAD - fo), (0, 0)))
    lay3 = lambda i: (i, 0, 0)
    h_lag = pl.pallas_call(
        _filter_mlp_kernel,
        grid=(n,),
        in_specs=[pl.BlockSpec((2 * L, FILTER_PAD), lambda i: (0, 0)),
                  pl.BlockSpec((1, FILTER_PAD, FILTER_PAD), lay3), pl.BlockSpec((1, 1, FILTER_PAD), lay3),
                  pl.BlockSpec((1, 1, FILTER_PAD), lay3), pl.BlockSpec((1, FILTER_PAD, FILTER_PAD), lay3),
                  pl.BlockSpec((1, 1, FILTER_PAD), lay3), pl.BlockSpec((1, 1, FILTER_PAD), lay3)],
        out_specs=pl.BlockSpec((1, 2 * L, FILTER_PAD), lay3),
        out_shape=jax.ShapeDtypeStruct((n, 2 * L, FILTER_PAD), F32),
        compiler_params=_cparams(("parallel",)),
        name="hyena_filter_mlp",
    )(z_lag, w1p, pad_v(b1), pad_v(fr1), w2p, pad_v(b2), pad_v(fr2))

    max_decay = math.log(HYENA_DECAY_TARGET) / HYENA_FAST_DECAY_PCT
    min_decay = math.log(HYENA_DECAY_TARGET) / HYENA_SLOW_DECAY_PCT
    deltas = jnp.linspace(min_decay, max_decay, C, dtype=F32).reshape(1, C)
    fa, fb, _ = _dft_constants(P)
    n_seg = 2 * L // P
    nct = C // ct
    return pl.pallas_call(
        functools.partial(_filter_spectrum_kernel, L=L, P=P),
        grid=(n, nct),
        in_specs=[pl.BlockSpec((1, 2 * L, FILTER_PAD), lambda i, c: (i, 0, 0)),
                  pl.BlockSpec((2 * L, 1), lambda i, c: (0, 0)),
                  pl.BlockSpec((1, ct), lambda i, c: (0, c)),
                  pl.BlockSpec((1, FILTER_PAD, ct), lambda i, c: (i, 0, c)),
                  pl.BlockSpec((1, FILTER_PAD, ct), lambda i, c: (i, 0, nct + c)),
                  pl.BlockSpec((2 * P, P), lambda i, c: (0, 0)),
                  pl.BlockSpec((2 * P, P), lambda i, c: (0, 0))],
        out_specs=pl.BlockSpec((1, n_seg - 1, 2 * P, ct), lambda i, c: (i, 0, 0, c)),
        out_shape=jax.ShapeDtypeStruct((n, n_seg - 1, 2 * P, C), F32),
        scratch_shapes=[pltpu.VMEM((2 * L, ct), BF16)],
        compiler_params=_cparams(("parallel", "parallel")),
        name="hyena_filter_spectrum",
    )(h_lag, t_lag, deltas, w3p, w3p, jnp.asarray(fa).astype(BF16), jnp.asarray(fb).astype(BF16))


def _hyena_kernel(u0_ref, u1_ref, u2_ref, cw0, cw1, cw2, cb0, cb1, cb2, bias_ref, g_ref, fa_ref, inv_ref,
                  o_ref, x0_scr, zv_scr, z_scr, y_scr, *, L, P, rc):
    nb = L // P
    row = lax.broadcasted_iota(jnp.int32, (L, 1), 0)

    def dwconv(u_ref, w_ref, b_ref):
        u = u_ref[0].astype(F32)
        w = w_ref[...]
        prev = jnp.where(row == 0, 0.0, pltpu.roll(u, 1, 0))
        nxt = jnp.where(row == L - 1, 0.0, pltpu.roll(u, L - 1, 0))
        return prev * w[0:1] + u * w[1:2] + nxt * w[2:3] + b_ref[...]

    x0_scr[...] = dwconv(u0_ref, cw0, cb0)
    zv_scr[...] = dwconv(u1_ref, cw1, cb1) * dwconv(u2_ref, cw2, cb2)
    for j in range(nb):
        z_scr[j] = _dot(fa_ref[...], zv_scr[j * P:(j + 1) * P, :].astype(BF16))

    def out_block(i, carry):
        def chunk(r, c2):
            re = pl.ds(pl.multiple_of(r * rc, rc), rc)
            im = pl.ds(pl.multiple_of(P + r * rc, rc), rc)
            acc_r = jnp.zeros((rc, o_ref.shape[-1]), F32)
            acc_i = jnp.zeros((rc, o_ref.shape[-1]), F32)
            for j in range(nb):
                d = i - j + (nb - 1)
                gr, gi = g_ref[0, d, re, :], g_ref[0, d, im, :]
                zr, zi = z_scr[j, re, :], z_scr[j, im, :]
                acc_r = acc_r + (gr * zr - gi * zi)
                acc_i = acc_i + (gr * zi + gi * zr)
            y_scr[re, :] = acc_r.astype(BF16)
            y_scr[im, :] = acc_i.astype(BF16)
            return c2

        lax.fori_loop(0, P // rc, chunk, 0)
        rows = pl.ds(pl.multiple_of(i * P, P), P)
        y = _dot(inv_ref[...], y_scr[...])
        zv = zv_scr[rows, :]
        o_ref[0, rows, :] = (x0_scr[rows, :] * (y + zv * bias_ref[...])).astype(o_ref.dtype)
        return carry

    lax.fori_loop(0, nb, out_block, 0)


def _hyena(u_hy, conv_w, conv_b, bias, spectra_i, layer, P, ct):
    B, L, _ = u_hy.shape
    C = D_HYENA
    nct = C // ct
    nb = L // P
    _, _, inv = _dft_constants(P)
    fa, _, _ = _dft_constants(P)
    cb = conv_b.reshape(1, 3 * C)
    col = lambda k: (lambda c, b: (b, 0, k * nct + c))
    wcol = lambda k: (lambda c, b: (0, k * nct + c))
    return pl.pallas_call(
        functools.partial(_hyena_kernel, L=L, P=P, rc=64),
        grid=(nct, B),
        in_specs=[pl.BlockSpec((1, L, ct), col(0)), pl.BlockSpec((1, L, ct), col(1)), pl.BlockSpec((1, L, ct), col(2)),
                  pl.BlockSpec((3, ct), wcol(0)), pl.BlockSpec((3, ct), wcol(1)), pl.BlockSpec((3, ct), wcol(2)),
                  pl.BlockSpec((1, ct), wcol(0)), pl.BlockSpec((1, ct), wcol(1)), pl.BlockSpec((1, ct), wcol(2)),
                  pl.BlockSpec((1, ct), lambda c, b: (0, c)),
                  pl.BlockSpec((1, 2 * nb - 1, 2 * P, ct), lambda c, b: (layer, 0, 0, c)),
                  pl.BlockSpec((2 * P, P), lambda c, b: (0, 0)),
                  pl.BlockSpec((P, 2 * P), lambda c, b: (0, 0))],
        out_specs=pl.BlockSpec((1, L, ct), lambda c, b: (b, 0, c)),
        out_shape=jax.ShapeDtypeStruct((B, L, C), BF16),
        scratch_shapes=[pltpu.VMEM((L, ct), F32), pltpu.VMEM((L, ct), F32),
                        pltpu.VMEM((nb, 2 * P, ct), F32), pltpu.VMEM((2 * P, ct), BF16)],
        compiler_params=_cparams(("parallel", "parallel")),
        name="hyena_conv",
    )(u_hy, u_hy, u_hy, conv_w, conv_w, conv_w, cb, cb, cb, bias.reshape(1, C), spectra_i,
      jnp.asarray(fa).astype(BF16), jnp.asarray(inv).astype(BF16))


def _split3(x):
    hi = x.astype(BF16)
    r = x - hi.astype(F32)
    mid = r.astype(BF16)
    lo = (r - mid.astype(F32)).astype(BF16)
    return hi, mid, lo


def _gla_kernel(q_ref, k_ref, v_ref, g_ref, lr_ref, w2_ref, gb_ref, ng_ref, o_ref, la_scr, acc_scr, *, L):
    C = GLA_CHUNK
    N = L // C
    R = GLA_GATE_RANK
    DK, DV = GLA_HEAD_K, GLA_HEAD_V
    hp = lax.Precision.HIGHEST

    lr = lr_ref[0]
    for z in range(2):
        gk = jnp.dot(lr[:, z * R:(z + 1) * R], w2_ref[z], precision=hp, preferred_element_type=F32) + gb_ref[z:z + 1, :]
        log_sig = jnp.minimum(gk, 0.0) - jnp.log1p(jnp.exp(-jnp.abs(gk)))
        la_scr[z] = jnp.maximum(log_sig / GLA_GATE_NORMALIZER, GLA_LOG_GATE_MIN)
    acc_scr[...] = jnp.zeros_like(acc_scr)

    r_i = lax.broadcasted_iota(jnp.int32, (C, C), 0)
    c_i = lax.broadcasted_iota(jnp.int32, (C, C), 1)
    tri_f = (r_i >= c_i)
    tri_b = (c_i > r_i)
    cum_f = tri_f.astype(BF16)
    cum_b = (c_i >= r_i).astype(BF16)
    scale = DK ** -0.5

    def chunk(rows, z, cum, mask, last_row, st):
        q = q_ref[0, rows, :].astype(F32) * scale
        k = k_ref[0, rows, :].astype(F32)
        v = v_ref[0, rows, :]
        hi, mid, lo = _split3(la_scr[z, rows, :])
        b = _dot(cum, hi) + _dot(cum, mid) + _dot(cum, lo)
        b_last = b[last_row:last_row + 1, :]
        qe = (q * jnp.exp(b)).astype(BF16)
        ke = k * jnp.exp(-b)
        dec = jnp.exp(b_last)
        kdec = (ke * dec).astype(BF16)
        s = lax.dot_general(qe, ke.astype(BF16), (((1,), (1,)), ((), ())), preferred_element_type=F32)
        s = jnp.where(mask, s, 0.0).astype(BF16)
        o = _dot(s, v) + lax.dot_general(qe, st.astype(BF16), (((1,), (1,)), ((), ())), preferred_element_type=F32)
        ut = lax.dot_general(v, kdec, (((0,), (0,)), ((), ())), preferred_element_type=F32)
        acc_scr[rows, :] += o
        return st * dec + ut

    def step(n, carry):
        sf, sb = carry
        rows_f = pl.ds(pl.multiple_of(n * C, C), C)
        rows_b = pl.ds(pl.multiple_of((N - 1 - n) * C, C), C)
        sf = chunk(rows_f, 0, cum_f, tri_f, C - 1, sf)
        sb = chunk(rows_b, 1, cum_b, tri_b, 0, sb)
        return sf, sb

    zero = jnp.zeros((DV, DK), F32)
    lax.fori_loop(0, N, step, (zero, zero))
    o = _rms(acc_scr[...], ng_ref[...])
    g = g_ref[0].astype(F32)
    o_ref[0] = (o * (g * jax.nn.sigmoid(g))).astype(o_ref.dtype)


def _gla(qkvg, lr, gate_w2, gate_b, norm_g):
    B, L, _ = qkvg.shape
    H = GLA_HEADS
    col = lambda k: (lambda b, h: (b, 0, k * H + h))
    return pl.pallas_call(
        functools.partial(_gla_kernel, L=L),
        grid=(B, H),
        in_specs=[pl.BlockSpec((1, L, GLA_HEAD_K), col(0)), pl.BlockSpec((1, L, GLA_HEAD_K), col(1)),
                  pl.BlockSpec((1, L, GLA_HEAD_V), col(2)), pl.BlockSpec((1, L, GLA_HEAD_V), col(3)),
                  pl.BlockSpec((1, L, 2 * GLA_GATE_RANK), lambda b, h: (b, 0, 0)),
                  pl.BlockSpec((2, GLA_GATE_RANK, GLA_HEAD_K), lambda b, h: (0, 0, h)),
                  pl.BlockSpec((2, GLA_HEAD_K), lambda b, h: (0, h)),
                  pl.BlockSpec((1, GLA_HEAD_V), lambda b, h: (0, 0))],
        out_specs=pl.BlockSpec((1, L, GLA_HEAD_V), lambda b, h: (b, 0, h)),
        out_shape=jax.ShapeDtypeStruct((B, L, GLA_VALUE_DIM), BF16),
        scratch_shapes=[pltpu.VMEM((2, L, GLA_HEAD_K), F32), pltpu.VMEM((L, GLA_HEAD_V), F32)],
        compiler_params=_cparams(("parallel", "parallel")),
        name="gla",
    )(qkvg, qkvg, qkvg, qkvg, lr, gate_w2, gate_b, norm_g.reshape(1, GLA_HEAD_V))


def _merge_kernel(x_ref, yh_ref, yg_ref, up_ref, upp_ref, upn_ref, gl_ref, pw_ref, ps_ref,
                  wh_ref, wg_ref, wp_ref, wo_ref, ng_ref, o_ref, *, L, tm):
    D = x_ref.shape[-1]
    H = POOL_HALO
    start = (pl.program_id(0) * tm) % L
    prev = jnp.where(start == 0, 0.0, upp_ref[...].astype(F32))
    nxt = jnp.where(start + tm == L, 0.0, upn_ref[...].astype(F32))
    main = up_ref[...].astype(F32)
    ext = jnp.concatenate([prev, main, nxt], axis=0)
    n_ext = tm + 2 * H
    pos = start + lax.broadcasted_iota(jnp.int32, (tm, 1), 0)
    ys = []
    for gi, w in enumerate(POOL_WINDOWS):
        half = w // 2
        cols = slice(gi * POOL_GROUP, (gi + 1) * POOL_GROUP)
        s = ext[:, cols]
        span = 1
        while span < w:
            s = s + pltpu.roll(s, span, 0)
            span *= 2
        if half > 1:
            s = pltpu.roll(s, n_ext - (half - 1), 0)
        wsum = s[H:H + tm]
        count = (jnp.minimum(pos + half, L) - jnp.maximum(pos - half, 0)).astype(F32)
        d = wsum / count - main[:, cols]
        ys.append(_dot(d.astype(BF16), pw_ref[gi]))
    y_pool = (jnp.concatenate(ys, axis=1) * ps_ref[...]).astype(BF16)

    gl = gl_ref[...].astype(F32)
    merged = (jax.nn.sigmoid(gl[:, 0:D]) * _dot(yh_ref[...], wh_ref[...])
              + jax.nn.sigmoid(gl[:, D:2 * D]) * _dot(yg_ref[...], wg_ref[...])
              + jax.nn.sigmoid(gl[:, 2 * D:3 * D]) * _dot(y_pool, wp_ref[...]))
    out = _dot(merged.astype(BF16), wo_ref[...])
    o_ref[...] = x_ref[...] + _rms(out, ng_ref[...])


def _merge(x2, y_hy, y_gla, u_pool, gate_logits, pool_w, pool_scale, w_h, w_g, w_p, w_o, norm_g, L, tm):
    T, D = x2.shape
    H = POOL_HALO
    nh = tm // H
    last = T // H - 1
    row = lambda i: (i, 0)
    const2 = lambda i: (0, 0)
    return pl.pallas_call(
        functools.partial(_merge_kernel, L=L, tm=tm),
        grid=(T // tm,),
        in_specs=[pl.BlockSpec((tm, D), row),
                  pl.BlockSpec((tm, D_HYENA), row), pl.BlockSpec((tm, GLA_VALUE_DIM), row),
                  pl.BlockSpec((tm, D_POOL), row),
                  pl.BlockSpec((H, D_POOL), lambda i: (jnp.maximum(i * nh - 1, 0), 0)),
                  pl.BlockSpec((H, D_POOL), lambda i: (jnp.minimum((i + 1) * nh, last), 0)),
                  pl.BlockSpec((tm, N_BRANCHES * D), row),
                  _resident((len(POOL_WINDOWS), POOL_GROUP, POOL_GROUP), lambda i: (0, 0, 0)),
                  _resident((1, D_POOL), const2),
                  _resident((D_HYENA, D), const2), _resident((GLA_VALUE_DIM, D), const2),
                  _resident((D_POOL, D), const2), _resident((D, D), const2), _resident((1, D), const2)],
        out_specs=pl.BlockSpec((tm, D), row),
        out_shape=jax.ShapeDtypeStruct((T, D), F32),
        compiler_params=_cparams(("parallel",)),
        name="merge",
    )(x2, y_hy, y_gla, u_pool, u_pool, u_pool, gate_logits, pool_w.astype(BF16), pool_scale.reshape(1, D_POOL),
      w_h.astype(BF16), w_g.astype(BF16), w_p.astype(BF16), w_o.astype(BF16), norm_g.reshape(1, D))


FFN_HALO = 8


def _ffn_kernel(x_ref, xp_ref, xn_ref, g1_ref, wa_ref, wb_ref, cw_ref, cb_ref, wd_ref, g2_ref, o_ref,
                *, L, tm, n_split):
    H = FFN_HALO
    start = (pl.program_id(0) * tm) % L
    x = x_ref[...]
    g1 = g1_ref[...]
    h_main = _rms(x, g1).astype(BF16)
    h_ext = _rms(jnp.concatenate([xp_ref[...], x, xn_ref[...]], axis=0), g1).astype(BF16)
    n_ext = tm + 2 * H
    r = lax.broadcasted_iota(jnp.int32, (tm, 1), 0)
    first = jnp.logical_and(start == 0, r == 0)
    lastr = jnp.logical_and(start + tm == L, r == tm - 1)
    F = wa_ref.shape[-1]
    fc = F // n_split
    y = jnp.zeros((tm, o_ref.shape[-1]), F32)
    for c in range(n_split):
        cols = slice(c * fc, (c + 1) * fc)
        a = _dot(h_ext, wa_ref[:, cols])
        b = _dot(h_main, wb_ref[:, cols])
        a_prev = jnp.where(first, 0.0, pltpu.roll(a, 1, 0)[H:H + tm])
        a_next = jnp.where(lastr, 0.0, pltpu.roll(a, n_ext - 1, 0)[H:H + tm])
        cw = cw_ref[:, cols]
        ac = a_prev * cw[0:1] + a[H:H + tm] * cw[1:2] + a_next * cw[2:3] + cb_ref[:, cols]
        gelu = 0.5 * ac * (1.0 + lax.erf(ac * (2.0 ** -0.5)))
        y = y + _dot((gelu * b).astype(BF16), wd_ref[cols, :])
    o_ref[...] = x + _rms(y, g2_ref[...])


def _ffn(x2, g1, w_up, conv_w, conv_b, w_down, g2, L, tm, n_split):
    T, D = x2.shape
    F = w_down.shape[0]
    H = FFN_HALO
    nh = tm // H
    last = T // H - 1
    row = lambda i: (i, 0)
    const = lambda i: (0, 0)
    wu = w_up.astype(BF16)
    return pl.pallas_call(
        functools.partial(_ffn_kernel, L=L, tm=tm, n_split=n_split),
        grid=(T // tm,),
        in_specs=[pl.BlockSpec((tm, D), row),
                  pl.BlockSpec((H, D), lambda i: (jnp.maximum(i * nh - 1, 0), 0)),
                  pl.BlockSpec((H, D), lambda i: (jnp.minimum((i + 1) * nh, last), 0)),
                  _resident((1, D), const), _resident((D, F), const), _resident((D, F), const),
                  _resident((3, F), const), _resident((1, F), const), _resident((F, D), const),
                  _resident((1, D), const)],
        out_specs=pl.BlockSpec((tm, D), row),
        out_shape=jax.ShapeDtypeStruct((T, D), F32),
        compiler_params=_cparams(("parallel",)),
        name="ffn",
    )(x2, x2, x2, g1.reshape(1, D), wu[:, :F], wu[:, F:], conv_w, conv_b.reshape(1, F),
      w_down.astype(BF16), g2.reshape(1, D))


def _positional_features(L):
    t = jnp.linspace(0.0, 1.0, L, dtype=F32)[:, None]
    bands = (HYENA_EMB_DIM - 1) // 2
    w = 2.0 * math.pi * jnp.arange(L, dtype=F32)[:, None] / L
    f = jnp.linspace(1e-4, bands - 1, bands, dtype=F32)[None, :]
    ang = f * w
    return t, jnp.concatenate([t, jnp.cos(ang), -jnp.sin(ang)], axis=-1)


def _tiles(L):
    tm = min(512, L)
    P = min(512, L)
    return tm, P, LANES


def kernel(x, norm_mix_pre, norm_mix_post, norm_ffn_pre, norm_ffn_post, w_in, hy_conv_w, hy_conv_b, hy_filt_w1, hy_filt_b1, hy_filt_freq1, hy_filt_w2, hy_filt_b2, hy_filt_freq2, hy_filt_w3, hy_bias, gla_gate_w2, gla_gate_b, gla_norm, pool_w, pool_scale, w_br_hyena, w_br_gla, w_br_pool, w_out, ffn_w_up, ffn_conv_w, ffn_conv_b, ffn_w_down):
    B, L, D = x.shape
    depth = w_in.shape[0]
    tm, P, ct = _tiles(L)
    assert L % tm == 0 and L % P == 0 and L % GLA_CHUNK == 0 and tm % POOL_HALO == 0
    t_pos, z_pos = _positional_features(L)
    spectra = _hyena_spectra(t_pos, z_pos, hy_filt_w1, hy_filt_b1, hy_filt_freq1, hy_filt_w2, hy_filt_b2,
                             hy_filt_freq2, hy_filt_w3, L, P, ct)
    x2 = x.reshape(B * L, D)
    for i in range(depth):
        u_hy, qkvg, lr, u_pool, gate_logits = _inproj(x2, norm_mix_pre[i], w_in[i], tm)
        y_hy = _hyena(u_hy.reshape(B, L, -1), hy_conv_w[i], hy_conv_b[i], hy_bias[i], spectra, i, P, ct)
        y_gla = _gla(qkvg.reshape(B, L, -1), lr.reshape(B, L, -1), gla_gate_w2[i], gla_gate_b[i], gla_norm[i])
        x2 = _merge(x2, y_hy.reshape(B * L, -1), y_gla.reshape(B * L, -1), u_pool, gate_logits, pool_w[i],
                    pool_scale[i], w_br_hyena[i], w_br_gla[i], w_br_pool[i], w_out[i], norm_mix_post[i], L, tm)
        x2 = _ffn(x2, norm_ffn_pre[i], ffn_w_up[i], ffn_conv_w[i], ffn_conv_b[i], ffn_w_down[i],
                  norm_ffn_post[i], L, tm, 2)
    return x2.reshape(B, L, D)
```
